```python
import math
import jax, jax.numpy as jnp
from jax import lax
import numpy as np

D_MODEL = 1024
BATCH = 8
SEQ = 2048
DEPTH = 4
DEC_BATCH = 128
DEC_SEQ = 1
PAST_LEN = 2048
PAGE_SIZE = 128

N_A_LAYERS = DEPTH // 2
N_B_LAYERS = DEPTH - N_A_LAYERS
S5_GROUP = 16
S5_GROUPS = D_MODEL // S5_GROUP
S5_STATE = 64
DT_MIN = 1e-3
DT_MAX = 1e-1
D_FF = 4 * D_MODEL
HEAD_DIM = 64
N_Q_HEADS = D_MODEL // HEAD_DIM
N_KV_HEADS = 4
Q_PER_KV = N_Q_HEADS // N_KV_HEADS
N_BRANCH = 3
CMP_BLOCK = 32
SEL_BLOCK = 64
N_SEL = 16
WINDOW = 512
CMP_HIDDEN = 4 * HEAD_DIM
Q_BLOCK = 64
ROPE_THETA = 10000.0
RMS_EPS = 1e-6
FORCE_SCORE = 1e3

kernel_name = 's5_nsa_yoco_decoder_step'


def _rms_norm(x, g):
    xf = x.astype(jnp.float32)
    y = xf * lax.rsqrt(jnp.mean(xf * xf, axis=-1, keepdims=True) + RMS_EPS)
    return (y * g.astype(jnp.float32)).astype(x.dtype)


def _rope(x, pos):
    half = HEAD_DIM // 2
    inv_freq = ROPE_THETA ** (-jnp.arange(half, dtype=jnp.float32) / half)
    ang = pos.astype(jnp.float32)[:, None] * inv_freq[None, :]
    cos = jnp.cos(ang)[:, None, :]
    sin = jnp.sin(ang)[:, None, :]
    xf = x.astype(jnp.float32)
    x1, x2 = xf[..., :half], xf[..., half:]
    return jnp.concatenate([x1 * cos - x2 * sin, x2 * cos + x1 * sin], axis=-1).astype(x.dtype)


def _masked_softmax(s, mask):
    s = jnp.where(mask, s, -jnp.inf)
    m = jnp.max(s, axis=-1, keepdims=True)
    m = jnp.where(jnp.isfinite(m), m, 0.0)
    p = jnp.exp(s - m)
    return p / jnp.maximum(jnp.sum(p, axis=-1, keepdims=True), 1e-30)


def _complex_affine_combine(e1, e2):
    a1r, a1i, b1r, b1i = e1
    a2r, a2i, b2r, b2i = e2
    return (a1r * a2r - a1i * a2i, a1r * a2i + a1i * a2r,
            a2r * b1r - a2i * b1i + b2r, a2r * b1i + a2i * b1r + b2i)


def _s5_mixer(h, s0, a_re, a_im, log_dt, b_re, b_im, c_re, c_im, d_skip, w_glu):
    nb, t, dm = h.shape
    f32 = jnp.float32
    u = h.astype(f32).reshape(nb, t, S5_GROUPS, S5_GROUP)
    dt = jnp.exp(log_dt.astype(f32))[:, None]
    ar, ai = a_re.astype(f32), a_im.astype(f32)
    mag = jnp.exp(ar * dt)
    abar_r, abar_i = mag * jnp.cos(ai * dt), mag * jnp.sin(ai * dt)
    den = ar * ar + ai * ai
    zoh_r = ((abar_r - 1.0) * ar + abar_i * ai) / den
    zoh_i = (abar_i * ar - (abar_r - 1.0) * ai) / den
    br, bi = b_re.astype(f32), b_im.astype(f32)
    bbar_r = zoh_r[..., None] * br - zoh_i[..., None] * bi
    bbar_i = zoh_r[..., None] * bi + zoh_i[..., None] * br
    bu_r = jnp.einsum('btgc,gpc->btgp', u, bbar_r)
    bu_i = jnp.einsum('btgc,gpc->btgp', u, bbar_i)
    a_r = jnp.broadcast_to(abar_r, (1, t) + abar_r.shape)
    a_i = jnp.broadcast_to(abar_i, (1, t) + abar_i.shape)
    acr, aci, bcr, bci = lax.associative_scan(_complex_affine_combine, (a_r, a_i, bu_r, bu_i), axis=1)
    s0r = s0[..., 0].astype(f32)[:, None]
    s0i = s0[..., 1].astype(f32)[:, None]
    sr = bcr + acr * s0r - aci * s0i
    si = bci + acr * s0i + aci * s0r
    y = (jnp.einsum('gcp,btgp->btgc', c_re.astype(f32), sr)
         - jnp.einsum('gcp,btgp->btgc', c_im.astype(f32), si))
    y = y.reshape(nb, t, dm) + d_skip.astype(f32) * h.astype(f32)
    z = jax.nn.gelu(y) @ w_glu.astype(f32)
    out = z[..., :dm] * jax.nn.sigmoid(z[..., dm:])
    return out.astype(h.dtype), jnp.stack([sr[:, -1], si[:, -1]], axis=-1)


def _sqrelu_mlp(h, w_up, w_down):
    return jnp.square(jax.nn.relu(h @ w_up)) @ w_down


def _kv_rows(x, pos, kv_norm, w_kv, k_norm):
    nb, t, _ = x.shape
    kv = (_rms_norm(x, kv_norm) @ w_kv).reshape(nb, t, N_BRANCH, 2, N_KV_HEADS, HEAD_DIM)

    def keyed(br):
        k = _rope(_rms_norm(kv[:, :, br, 0], k_norm[br]), pos)
        return jnp.stack([k, kv[:, :, br, 1]], axis=2)
    return kv[:, :, 0], keyed(1), keyed(2)


def _compress(rows, cmp_pos, cmp_w1, cmp_w2, k_gain):
    nb, t = rows.shape[:2]
    n_blk = t // CMP_BLOCK
    blk = rows[:, :n_blk * CMP_BLOCK].reshape(nb, n_blk, CMP_BLOCK, 2, N_KV_HEADS, HEAD_DIM)
    blk = blk + cmp_pos.transpose(1, 0, 2)[:, :, None, :]
    w1 = cmp_w1.reshape(2, CMP_BLOCK, HEAD_DIM, CMP_HIDDEN)
    hid = jax.nn.gelu(jnp.einsum('bnlsgd,sldh->bnsgh', blk, w1))
    comp = jnp.einsum('bnsgh,she->bnsge', hid, cmp_w2)
    c_end = (jnp.arange(n_blk) + 1) * CMP_BLOCK - 1
    kc = _rope(_rms_norm(comp[:, :, 0], k_gain), c_end)
    return kc, comp[:, :, 1], c_end


def _nsa_attend(q, gates, q_pos, kc, vc, c_end, gather_sel, n_sel_blocks, kw, vw, kw_pos):
    nb, nq = q.shape[:2]
    f32 = jnp.float32
    qg = q.astype(f32).reshape(nb, nq, N_KV_HEADS, Q_PER_KV, HEAD_DIM) * (HEAD_DIM ** -0.5)
    s_c = jnp.einsum('bqgrd,bngd->bqgrn', qg, kc.astype(f32))
    p_c = _masked_softmax(s_c, (c_end[None, :] <= q_pos[:, None])[None, :, None, None, :])
    o_c = jnp.einsum('bqgrn,bngd->bqgrd', p_c, vc.astype(f32))
    ratio = SEL_BLOCK // CMP_BLOCK
    imp = p_c.sum(axis=3)
    imp = jnp.pad(imp, ((0, 0), (0, 0), (0, 0), (0, n_sel_blocks * ratio - imp.shape[-1])))
    imp = imp.reshape(nb, nq, N_KV_HEADS, n_sel_blocks, ratio).sum(-1)
    cur = q_pos // SEL_BLOCK
    j = jnp.arange(n_sel_blocks)[None, :]
    causal_blk = j <= cur[:, None]
    forced = (j == 0) | (j == cur[:, None]) | (j == cur[:, None] - 1)
    score = jnp.where(causal_blk[None, :, None], imp + FORCE_SCORE * forced[None, :, None], -jnp.inf)
    n_top = min(N_SEL, n_sel_blocks)
    _, top_idx = lax.top_k(score, n_top)
    ks, vs = gather_sel(top_idx)
    tok_pos = top_idx[..., None] * SEL_BLOCK + jnp.arange(SEL_BLOCK)
    sel_mask = (tok_pos <= q_pos[None, :, None, None, None]).reshape(nb, nq, N_KV_HEADS, 1, n_top * SEL_BLOCK)
    s_s = jnp.einsum('bqgrd,bqgkld->bqgrkl', qg, ks.astype(f32)).reshape(nb, nq, N_KV_HEADS, Q_PER_KV, n_top * SEL_BLOCK)
    p_s = _masked_softmax(s_s, sel_mask)
    o_s = jnp.einsum('bqgrm,bqgmd->bqgrd', p_s, vs.astype(f32).reshape(nb, nq, N_KV_HEADS, n_top * SEL_BLOCK, HEAD_DIM))
    dist = q_pos[:, None] - kw_pos[None, :]
    w_mask = (dist >= 0) & (dist < WINDOW) & (kw_pos[None, :] >= 0)
    s_w = jnp.einsum('bqgrd,blgd->bqgrl', qg, kw.astype(f32))
    p_w = _masked_softmax(s_w, w_mask[None, :, None, None, :])
    o_w = jnp.einsum('bqgrl,blgd->bqgrd', p_w, vw.astype(f32))
    g = gates.astype(f32).reshape(nb, nq, N_KV_HEADS, Q_PER_KV, N_BRANCH)
    o = g[..., 0:1] * o_c + g[..., 1:2] * o_s + g[..., 2:3] * o_w
    return o.reshape(nb, nq, N_Q_HEADS * HEAD_DIM).astype(q.dtype)


def _nsa_queries(x, pos, norm_g, w_in, q_gain):
    nb, t, _ = x.shape
    proj = _rms_norm(x, norm_g) @ w_in
    qd = N_Q_HEADS * HEAD_DIM
    q = _rope(_rms_norm(proj[..., :qd].reshape(nb, t, N_Q_HEADS, HEAD_DIM), q_gain), pos)
    gates = jax.nn.sigmoid(proj[..., qd:].astype(jnp.float32)).reshape(nb, t, N_Q_HEADS, N_BRANCH)
    return q, gates


def _prompt_mixer_b(x, pos, kv_norm, w_kv, k_norm, cmp_pos, cmp_w1, cmp_w2):
    nb, t, _ = x.shape
    cmp_rows, slc_rows, win_rows = _kv_rows(x, pos, kv_norm, w_kv, k_norm)
    kc, vc, c_end = _compress(cmp_rows, cmp_pos, cmp_w1, cmp_w2, k_norm[0])
    n_sel_blk = t // SEL_BLOCK
    sel_blocks = slc_rows.reshape(nb, n_sel_blk, SEL_BLOCK, 2, N_KV_HEADS, HEAD_DIM)
    win_pad = jnp.pad(win_rows, ((0, 0), (WINDOW, 0), (0, 0), (0, 0), (0, 0)))
    b_ar = jnp.arange(nb)[:, None, None, None]
    g_ar = jnp.arange(N_KV_HEADS)[None, None, :, None]

    def gather(idx):
        blk = sel_blocks[b_ar, idx, :, :, g_ar]
        return blk[..., 0, :], blk[..., 1, :]

    def mixer(q, gates):
        def one_block(i):
            start = i * Q_BLOCK
            qb = lax.dynamic_slice_in_dim(q, start, Q_BLOCK, axis=1)
            gb = lax.dynamic_slice_in_dim(gates, start, Q_BLOCK, axis=1)
            wb = lax.dynamic_slice_in_dim(win_pad, start, WINDOW + Q_BLOCK, axis=1)
            q_pos = start + jnp.arange(Q_BLOCK)
            kw_pos = start - WINDOW + jnp.arange(WINDOW + Q_BLOCK)
            return _nsa_attend(qb, gb, q_pos, kc, vc, c_end, gather, n_sel_blk, wb[:, :, 0], wb[:, :, 1], kw_pos)
        out = lax.map(one_block, jnp.arange(t // Q_BLOCK))
        return out.transpose(1, 0, 2, 3).reshape(nb, t, N_Q_HEADS * HEAD_DIM)

    return mixer, (cmp_rows, slc_rows, win_rows[:, -min(WINDOW, t):])


def _sample_mixer_b(x, pos, cache_cmp_kv, cache_slc_kv, cache_win_kv, page_table,
                    kv_norm, w_kv, k_norm, cmp_pos, cmp_w1, cmp_w2):
    nb, t, _ = x.shape
    past_len = page_table.shape[1] * PAGE_SIZE
    row_shape = (2, N_KV_HEADS, HEAD_DIM)
    cmp_new, slc_new, win_new = _kv_rows(x, pos, kv_norm, w_kv, k_norm)
    past_cmp = cache_cmp_kv[page_table].reshape((nb, past_len) + row_shape)
    kc, vc, c_end = _compress(jnp.concatenate([past_cmp, cmp_new], axis=1), cmp_pos, cmp_w1, cmp_w2, k_norm[0])
    bpp = PAGE_SIZE // SEL_BLOCK
    nb_past = past_len // SEL_BLOCK
    nb_new = -(-t // SEL_BLOCK)
    pool_blocks = cache_slc_kv.reshape((-1, SEL_BLOCK) + row_shape)
    j_past = jnp.arange(nb_past)
    phys_tbl = page_table[:, j_past // bpp] * bpp + j_past % bpp
    new_blocks = jnp.pad(slc_new, ((0, 0), (0, nb_new * SEL_BLOCK - t), (0, 0), (0, 0), (0, 0)))
    new_blocks = new_blocks.reshape((nb, nb_new, SEL_BLOCK) + row_shape)
    b_ar = jnp.arange(nb)[:, None, None, None]
    g_ar = jnp.arange(N_KV_HEADS)[None, None, :, None]

    def gather(idx):
        is_past = (idx < nb_past)[..., None, None, None]
        phys = phys_tbl[b_ar, jnp.minimum(idx, nb_past - 1)]
        from_past = pool_blocks[phys, :, :, g_ar]
        from_new = new_blocks[b_ar, jnp.clip(idx - nb_past, 0, nb_new - 1), :, :, g_ar]
        blk = jnp.where(is_past, from_past, from_new)
        return blk[..., 0, :], blk[..., 1, :]

    win_buf = cache_win_kv.shape[1]
    win_all = jnp.concatenate([cache_win_kv, win_new], axis=1)
    kw_pos = past_len - win_buf + jnp.arange(win_buf + t)

    def mixer(q, gates):
        return _nsa_attend(q, gates, pos, kc, vc, c_end, gather, nb_past + nb_new,
                           win_all[:, :, 0], win_all[:, :, 1], kw_pos)

    return mixer, (cmp_new, slc_new, win_all[:, -win_buf:])


def _trunk(x, pos, s5_init, build_mixer_b, norm_mix, norm_mlp, w_up, w_down, s5_params,
           nsa_w_in, q_norm, nsa_w_o):
    s5_final = []
    mixer_b, new_rows = None, None
    for layer in range(DEPTH):
        if layer < N_A_LAYERS:
            out, s_last = _s5_mixer(_rms_norm(x, norm_mix[layer]), s5_init[layer],
                                    *[p[layer] for p in s5_params])
            s5_final.append(s_last)
        else:
            if layer == N_A_LAYERS:
                mixer_b, new_rows = build_mixer_b(x)
            jb = layer - N_A_LAYERS
            q, gates = _nsa_queries(x, pos, norm_mix[layer], nsa_w_in[jb], q_norm[jb])
            out = mixer_b(q, gates) @ nsa_w_o[jb]
        x = x + out
        x = x + _sqrelu_mlp(_rms_norm(x, norm_mlp[layer]), w_up[layer], w_down[layer])
    return x, new_rows, jnp.stack(s5_final)


def setup_inputs(seed: int = 0) -> dict:
    key = jax.random.key(seed)
    ks = iter(jax.random.split(key, 40))
    f32 = jnp.float32

    def nrm(shape, scale):
        return scale * jax.random.normal(next(ks), shape, f32)

    n_pages = PAST_LEN // PAGE_SIZE
    n_pool = (5 * DEC_BATCH * n_pages + 3) // 4
    win_buf = min(WINDOW, PAST_LEN)
    row = (2, N_KV_HEADS, HEAD_DIM)
    inp = {}
    inp['x_prompt'] = nrm((BATCH, SEQ, D_MODEL), 1.0)
    inp['x_sample'] = nrm((DEC_BATCH, DEC_SEQ, D_MODEL), 1.0)
    inp['cache_cmp_kv'] = nrm((n_pool, PAGE_SIZE) + row, 1.0)
    inp['cache_slc_kv'] = nrm((n_pool, PAGE_SIZE) + row, 1.0)
    inp['cache_win_kv'] = nrm((DEC_BATCH, win_buf) + row, 1.0)
    inp['state_s5'] = nrm((N_A_LAYERS, DEC_BATCH, S5_GROUPS, S5_STATE, 2), 0.3)
    perm = jax.random.permutation(next(ks), n_pool)
    inp['page_table'] = perm[:DEC_BATCH * n_pages].reshape(DEC_BATCH, n_pages).astype(jnp.int32)
    inp['norm_mix'] = 1.0 + nrm((DEPTH, D_MODEL), 0.05)
    inp['norm_mlp'] = 1.0 + nrm((DEPTH, D_MODEL), 0.05)
    inp['w_up'] = nrm((DEPTH, D_MODEL, D_FF), D_MODEL ** -0.5)
    inp['w_down'] = nrm((DEPTH, D_FF, D_MODEL), D_FF ** -0.5)
    inp['s5_a_re'] = -0.5 + nrm((N_A_LAYERS, S5_GROUPS, S5_STATE), 0.01)
    inp['s5_a_im'] = math.pi * jnp.arange(S5_STATE, dtype=f32) + nrm((N_A_LAYERS, S5_GROUPS, S5_STATE), 0.01)
    u = jax.random.uniform(next(ks), (N_A_LAYERS, S5_GROUPS), f32)
    inp['s5_log_dt'] = math.log(DT_MIN) + u * (math.log(DT_MAX) - math.log(DT_MIN))
    inp['s5_b_re'] = nrm((N_A_LAYERS, S5_GROUPS, S5_STATE, S5_GROUP), S5_GROUP ** -0.5)
    inp['s5_b_im'] = nrm((N_A_LAYERS, S5_GROUPS, S5_STATE, S5_GROUP), S5_GROUP ** -0.5)
    inp['s5_c_re'] = nrm((N_A_LAYERS, S5_GROUPS, S5_GROUP, S5_STATE), (2 * S5_STATE) ** -0.5)
    inp['s5_c_im'] = nrm((N_A_LAYERS, S5_GROUPS, S5_GROUP, S5_STATE), (2 * S5_STATE) ** -0.5)
    inp['s5_d'] = nrm((N_A_LAYERS, D_MODEL), 0.5)
    inp['s5_w_glu'] = nrm((N_A_LAYERS, D_MODEL, 2 * D_MODEL), D_MODEL ** -0.5)
    inp['kv_norm'] = 1.0 + nrm((D_MODEL,), 0.05)
    inp['w_kv'] = nrm((D_MODEL, N_BRANCH * 2 * N_KV_HEADS * HEAD_DIM), D_MODEL ** -0.5)
    inp['k_norm'] = 1.0 + nrm((N_BRANCH, HEAD_DIM), 0.05)
    inp['cmp_pos'] = nrm((2, CMP_BLOCK, HEAD_DIM), 0.1)
    inp['cmp_w1'] = nrm((2, CMP_BLOCK * HEAD_DIM, CMP_HIDDEN), (CMP_BLOCK * HEAD_DIM) ** -0.5)
    inp['cmp_w2'] = nrm((2, CMP_HIDDEN, HEAD_DIM), CMP_HIDDEN ** -0.5)
    inp['nsa_w_in'] = nrm((N_B_LAYERS, D_MODEL, N_Q_HEADS * HEAD_DIM + N_BRANCH * N_Q_HEADS), D_MODEL ** -0.5)
    inp['q_norm'] = 1.0 + nrm((N_B_LAYERS, HEAD_DIM), 0.05)
    inp['nsa_w_o'] = nrm((N_B_LAYERS, N_Q_HEADS * HEAD_DIM, D_MODEL), (N_Q_HEADS * HEAD_DIM) ** -0.5)
    return inp


def reference(x_prompt, x_sample, cache_cmp_kv, cache_slc_kv, cache_win_kv, state_s5, page_table,
              norm_mix, norm_mlp, w_up, w_down,
              s5_a_re, s5_a_im, s5_log_dt, s5_b_re, s5_b_im, s5_c_re, s5_c_im, s5_d, s5_w_glu,
              kv_norm, w_kv, k_norm, cmp_pos, cmp_w1, cmp_w2, nsa_w_in, q_norm, nsa_w_o):
    s5_params = (s5_a_re, s5_a_im, s5_log_dt, s5_b_re, s5_b_im, s5_c_re, s5_c_im, s5_d, s5_w_glu)
    past_len = page_table.shape[1] * PAGE_SIZE
    pos_p = jnp.arange(x_prompt.shape[1])
    pos_s = past_len + jnp.arange(x_sample.shape[1])
    s5_zero = jnp.zeros((N_A_LAYERS, x_prompt.shape[0], S5_GROUPS, S5_STATE, 2), jnp.float32)

    def build_prompt(h):
        return _prompt_mixer_b(h, pos_p, kv_norm, w_kv, k_norm, cmp_pos, cmp_w1, cmp_w2)

    def build_sample(h):
        return _sample_mixer_b(h, pos_s, cache_cmp_kv, cache_slc_kv, cache_win_kv, page_table,
                               kv_norm, w_kv, k_norm, cmp_pos, cmp_w1, cmp_w2)

    y_prompt, rows_p, s5_prompt = _trunk(x_prompt, pos_p, s5_zero, build_prompt, norm_mix, norm_mlp,
                                         w_up, w_down, s5_params, nsa_w_in, q_norm, nsa_w_o)
    y_sample, rows_s, s5_sample = _trunk(x_sample, pos_s, state_s5, build_sample, norm_mix, norm_mlp,
                                         w_up, w_down, s5_params, nsa_w_in, q_norm, nsa_w_o)
    cmp_kv_prompt, slc_kv_prompt, win_kv_prompt = rows_p
    cmp_kv_sample, slc_kv_sample, win_kv_sample = rows_s
    return (y_prompt, y_sample, cmp_kv_prompt, cmp_kv_sample, slc_kv_prompt, slc_kv_sample,
            win_kv_prompt, win_kv_sample, s5_prompt, s5_sample)
```

```python
import functools
import math

import jax
import jax.numpy as jnp
from jax import lax
from jax.experimental import pallas as pl
from jax.experimental.pallas import tpu as pltpu

F32 = jnp.float32
BF16 = jnp.bfloat16
HIGHEST = lax.Precision.HIGHEST

D_MODEL = 1024
S5_GROUP = 16
S5_GROUPS = D_MODEL // S5_GROUP
S5_STATE = 64
D_FF = 4 * D_MODEL
HEAD_DIM = 64
N_Q_HEADS = D_MODEL // HEAD_DIM
N_KV_HEADS = 4
Q_PER_KV = N_Q_HEADS // N_KV_HEADS
N_BRANCH = 3
CMP_BLOCK = 32
SEL_BLOCK = 64
N_SEL = 16
WINDOW = 512
CMP_HIDDEN = 4 * HEAD_DIM
PAGE_SIZE = 128
ROPE_THETA = 10000.0
RMS_EPS = 1e-6
FORCE_SCORE = 1e3
KV_WIDTH = N_KV_HEADS * HEAD_DIM
ROW_WIDTH = 2 * KV_WIDTH

S5_CHUNK = 64
S5_CHUNK_COLS = S5_CHUNK * S5_GROUP
S5_SAMPLE_GROUPS_PER_STEP = 8
NEG_BIG = -1e30
VMEM_LIMIT = 56 * 1024 * 1024

NT_DIMS = (((1,), (1,)), ((), ()))
TN_DIMS = (((0,), (0,)), ((), ()))


def _cparams(*sem):
    return pltpu.CompilerParams(dimension_semantics=sem, vmem_limit_bytes=VMEM_LIMIT)


def _rms(x, g):
    ms = jnp.mean(x * x, axis=-1, keepdims=True)
    return x * lax.rsqrt(ms + RMS_EPS) * g


def _gelu(x):
    return jax.nn.gelu(x)


def _seg_mean_sq(x, seg):
    x2 = x * x
    hi = x2.astype(BF16)
    lo = (x2 - hi.astype(F32)).astype(BF16)
    s = jnp.dot(hi, seg, preferred_element_type=F32) + jnp.dot(lo, seg, preferred_element_type=F32)
    return s * (1.0 / HEAD_DIM)


def _head_norm_rope(x, gain, cos, sin_signed, seg):
    y = x * lax.rsqrt(_seg_mean_sq(x, seg) + RMS_EPS) * gain
    lane = lax.broadcasted_iota(jnp.int32, y.shape, 1)
    first_half = (lane % HEAD_DIM) < (HEAD_DIM // 2)
    width = y.shape[1]
    rot = jnp.where(first_half, pltpu.roll(y, width - HEAD_DIM // 2, 1), pltpu.roll(y, HEAD_DIM // 2, 1))
    cos_w = jnp.concatenate([cos] * (width // cos.shape[1]), axis=1)
    sin_w = jnp.concatenate([sin_signed] * (width // sin_signed.shape[1]), axis=1)
    return y * cos_w + rot * sin_w


def _norm_kernel(x_ref, g_ref, o_ref):
    o_ref[...] = _rms(x_ref[...], g_ref[...]).astype(o_ref.dtype)


def _norm_rows(x, g, tm, dtype):
    n = x.shape[0]
    return pl.pallas_call(
        _norm_kernel,
        grid=(n // tm,),
        in_specs=[pl.BlockSpec((tm, D_MODEL), lambda i: (i, 0)),
                  pl.BlockSpec((1, D_MODEL), lambda i: (0, 0))],
        out_specs=pl.BlockSpec((tm, D_MODEL), lambda i: (i, 0)),
        out_shape=jax.ShapeDtypeStruct((n, D_MODEL), dtype),
        compiler_params=_cparams("parallel"),
        name="norm_rows",
    )(x, g.reshape(1, D_MODEL))


def _mlp_kernel(x_ref, g_ref, wu_ref, wd_ref, o_ref, h_ref, acc_ref):
    f = pl.program_id(1)

    @pl.when(f == 0)
    def _():
        h_ref[...] = _rms(x_ref[...], g_ref[...]).astype(BF16)
        acc_ref[...] = jnp.zeros_like(acc_ref)

    u = jnp.dot(h_ref[...], wu_ref[...], preferred_element_type=F32)
    u = jnp.square(jnp.maximum(u, 0.0)).astype(BF16)
    acc_ref[...] += jnp.dot(u, wd_ref[...], preferred_element_type=F32)

    @pl.when(f == pl.num_programs(1) - 1)
    def _():
        o_ref[...] = x_ref[...] + acc_ref[...]


def _mlp(x, g, w_up, w_down, tm, tf):
    n = x.shape[0]
    return pl.pallas_call(
        _mlp_kernel,
        grid=(n // tm, D_FF // tf),
        in_specs=[pl.BlockSpec((tm, D_MODEL), lambda i, f: (i, 0)),
                  pl.BlockSpec((1, D_MODEL), lambda i, f: (0, 0)),
                  pl.BlockSpec((D_MODEL, tf), lambda i, f: (0, f)),
                  pl.BlockSpec((tf, D_MODEL), lambda i, f: (f, 0))],
        out_specs=pl.BlockSpec((tm, D_MODEL), lambda i, f: (i, 0)),
        out_shape=jax.ShapeDtypeStruct((n, D_MODEL), F32),
        scratch_shapes=[pltpu.VMEM((tm, D_MODEL), BF16), pltpu.VMEM((tm, D_MODEL), F32)],
        compiler_params=_cparams("parallel", "arbitrary"),
        name="mlp",
    )(x, g.reshape(1, D_MODEL), w_up, w_down)


def _proj_res_kernel(a_ref, w_ref, r_ref, o_ref):
    o_ref[...] = r_ref[...] + jnp.dot(a_ref[...], w_ref[...], preferred_element_type=F32)


def _proj_residual(a, w, res, tm):
    n, k = a.shape
    return pl.pallas_call(
        _proj_res_kernel,
        grid=(n // tm,),
        in_specs=[pl.BlockSpec((tm, k), lambda i: (i, 0)),
                  pl.BlockSpec((k, D_MODEL), lambda i: (0, 0)),
                  pl.BlockSpec((tm, D_MODEL), lambda i: (i, 0))],
        out_specs=pl.BlockSpec((tm, D_MODEL), lambda i: (i, 0)),
        out_shape=jax.ShapeDtypeStruct((n, D_MODEL), F32),
        compiler_params=_cparams("parallel"),
        name="proj_residual",
    )(a, w, res)


def _s5_out_kernel(x_ref, y_ref, g_ref, d_ref, w_ref, o_ref):
    x = x_ref[...]
    h = _rms(x, g_ref[...])
    y = y_ref[...].astype(F32) + d_ref[...] * h
    z = jnp.dot(_gelu(y).astype(BF16), w_ref[...], preferred_element_type=F32)
    o_ref[...] = x + z[:, :D_MODEL] * jax.nn.sigmoid(z[:, D_MODEL:])


def _s5_out(x, y, g, d_skip, w_glu, tm):
    n = x.shape[0]
    return pl.pallas_call(
        _s5_out_kernel,
        grid=(n // tm,),
        in_specs=[pl.BlockSpec((tm, D_MODEL), lambda i: (i, 0)),
                  pl.BlockSpec((tm, D_MODEL), lambda i: (i, 0)),
                  pl.BlockSpec((1, D_MODEL), lambda i: (0, 0)),
                  pl.BlockSpec((1, D_MODEL), lambda i: (0, 0)),
                  pl.BlockSpec((D_MODEL, 2 * D_MODEL), lambda i: (0, 0))],
        out_specs=pl.BlockSpec((tm, D_MODEL), lambda i: (i, 0)),
        out_shape=jax.ShapeDtypeStruct((n, D_MODEL), F32),
        compiler_params=_cparams("parallel"),
        name="s5_out",
    )(x, y, g.reshape(1, D_MODEL), d_skip.reshape(1, D_MODEL), w_glu)


def _kv_kernel(x_ref, g_ref, w_ref, kn_ref, cos_ref, sin_ref, seg_ref,
               cmp_ref, slc_ref, win_ref, sk_ref, sv_ref, wk_ref, wv_ref):
    h = _rms(x_ref[...], g_ref[...]).astype(BF16)
    kv = jnp.dot(h, w_ref[...], preferred_element_type=F32)
    cmp_ref[...] = kv[:, :ROW_WIDTH]
    for br, (row_ref, k_ref, v_ref) in enumerate(((slc_ref, sk_ref, sv_ref), (win_ref, wk_ref, wv_ref))):
        base = (br + 1) * ROW_WIDTH
        k = _head_norm_rope(kv[:, base:base + KV_WIDTH], kn_ref[br:br + 1, :],
                            cos_ref[...], sin_ref[...], seg_ref[...])
        v = kv[:, base + KV_WIDTH:base + ROW_WIDTH]
        row_ref[:, :KV_WIDTH] = k
        row_ref[:, KV_WIDTH:] = v
        for gi in range(N_KV_HEADS):
            k_ref[0, gi] = k[:, gi * HEAD_DIM:(gi + 1) * HEAD_DIM].astype(BF16)
            v_ref[0, gi] = v[:, gi * HEAD_DIM:(gi + 1) * HEAD_DIM].astype(BF16)


def _kv_rows(x, nb, t, kv_norm, w_kv, k_norm, cos_tab, sin_tab, seg, tm):
    n = x.shape[0]
    tiles_per_seq = t // tm
    kn = jnp.tile(k_norm[1:3], (1, N_KV_HEADS))
    row = jax.ShapeDtypeStruct((n, ROW_WIDTH), F32)
    hm = jax.ShapeDtypeStruct((nb, N_KV_HEADS, t, HEAD_DIM), BF16)
    row_spec = pl.BlockSpec((tm, ROW_WIDTH), lambda i: (i, 0))
    hm_spec = pl.BlockSpec((1, N_KV_HEADS, tm, HEAD_DIM),
                           lambda i: (i // tiles_per_seq, 0, i % tiles_per_seq, 0))
    tab_spec = pl.BlockSpec((tm, 2 * HEAD_DIM), lambda i: (i % tiles_per_seq, 0))
    return pl.pallas_call(
        _kv_kernel,
        grid=(n // tm,),
        in_specs=[pl.BlockSpec((tm, D_MODEL), lambda i: (i, 0)),
                  pl.BlockSpec((1, D_MODEL), lambda i: (0, 0)),
                  pl.BlockSpec((D_MODEL, N_BRANCH * ROW_WIDTH), lambda i: (0, 0)),
                  pl.BlockSpec((2, KV_WIDTH), lambda i: (0, 0)),
                  tab_spec, tab_spec,
                  pl.BlockSpec((KV_WIDTH, KV_WIDTH), lambda i: (0, 0))],
        out_specs=[row_spec, row_spec, row_spec, hm_spec, hm_spec, hm_spec, hm_spec],
        out_shape=[row, row, row, hm, hm, hm, hm],
        compiler_params=_cparams("parallel"),
        name="kv_rows",
    )(x, kv_norm.reshape(1, D_MODEL), w_kv, kn, cos_tab, sin_tab, seg)


def _q_kernel(x_ref, g_ref, wq_ref, wg_ref, qn_ref, cos_ref, sin_ref, seg_ref, q_ref, gate_ref):
    h = _rms(x_ref[...], g_ref[...]).astype(BF16)
    qq = jnp.dot(h, wq_ref[...], preferred_element_type=F32)
    for c in range(N_KV_HEADS):
        q = _head_norm_rope(qq[:, c * KV_WIDTH:(c + 1) * KV_WIDTH], qn_ref[...],
                            cos_ref[...], sin_ref[...], seg_ref[...]) * (HEAD_DIM ** -0.5)
        for r in range(Q_PER_KV):
            q_ref[0, c * Q_PER_KV + r] = q[:, r * HEAD_DIM:(r + 1) * HEAD_DIM]
    gate_ref[...] = jax.nn.sigmoid(jnp.dot(h, wg_ref[...], preferred_element_type=F32))


def _q_rows(x, nb, t, norm_g, w_q, w_g, q_gain, cos_tab, sin_tab, seg, tm):
    n = x.shape[0]
    tiles_per_seq = t // tm
    qn = jnp.tile(q_gain.reshape(1, HEAD_DIM), (1, N_KV_HEADS))
    tab_spec = pl.BlockSpec((tm, 2 * HEAD_DIM), lambda i: (i % tiles_per_seq, 0))
    return pl.pallas_call(
        _q_kernel,
        grid=(n // tm,),
        in_specs=[pl.BlockSpec((tm, D_MODEL), lambda i: (i, 0)),
                  pl.BlockSpec((1, D_MODEL), lambda i: (0, 0)),
                  pl.BlockSpec((D_MODEL, D_MODEL), lambda i: (0, 0)),
                  pl.BlockSpec((D_MODEL, 128), lambda i: (0, 0)),
                  pl.BlockSpec((1, KV_WIDTH), lambda i: (0, 0)),
                  tab_spec, tab_spec,
                  pl.BlockSpec((KV_WIDTH, KV_WIDTH), lambda i: (0, 0))],
        out_specs=[pl.BlockSpec((1, N_Q_HEADS, tm, HEAD_DIM),
                                lambda i: (i // tiles_per_seq, 0, i % tiles_per_seq, 0)),
                   pl.BlockSpec((tm, 128), lambda i: (i, 0))],
        out_shape=[jax.ShapeDtypeStruct((nb, N_Q_HEADS, t, HEAD_DIM), F32),
                   jax.ShapeDtypeStruct((n, 128), F32)],
        compiler_params=_cparams("parallel"),
        name="q_rows",
    )(x, norm_g.reshape(1, D_MODEL), w_q, w_g, qn, cos_tab, sin_tab, seg)


def _s5_discretize(a_re, a_im, log_dt, b_re, b_im):
    dt = jnp.exp(log_dt)[:, None]
    mag = jnp.exp(a_re * dt)
    abar_r, abar_i = mag * jnp.cos(a_im * dt), mag * jnp.sin(a_im * dt)
    den = a_re * a_re + a_im * a_im
    zoh_r = ((abar_r - 1.0) * a_re + abar_i * a_im) / den
    zoh_i = (abar_i * a_re - (abar_r - 1.0) * a_im) / den
    bbar_r = zoh_r[..., None] * b_re - zoh_i[..., None] * b_im
    bbar_i = zoh_r[..., None] * b_im + zoh_i[..., None] * b_re
    return dt, abar_r, abar_i, bbar_r, bbar_i


def _s5_chunk_operators(a_re, a_im, log_dt, b_re, b_im, c_re, c_im):
    ln = S5_CHUNK
    dt, _, _, bbar_r, bbar_i = _s5_discretize(a_re, a_im, log_dt, b_re, b_im)
    tau = jnp.arange(ln + 1, dtype=F32)[:, None, None]
    emag = jnp.exp(tau * (a_re * dt)[None])
    ang = tau * (a_im * dt)[None]
    e_r, e_i = emag * jnp.cos(ang), emag * jnp.sin(ang)
    cb_r = c_re[..., None] * bbar_r[:, None] - c_im[..., None] * bbar_i[:, None]
    cb_i = c_re[..., None] * bbar_i[:, None] + c_im[..., None] * bbar_r[:, None]
    kern = (jnp.einsum('tgp,gapc->gtac', e_r[:ln], cb_r, precision=HIGHEST)
            - jnp.einsum('tgp,gapc->gtac', e_i[:ln], cb_i, precision=HIGHEST))
    kflat = kern.transpose(0, 3, 1, 2).reshape(S5_GROUPS, S5_GROUP, S5_CHUNK_COLS)
    rev_r = e_r[ln - 1::-1].transpose(1, 0, 2)[:, :, None, :]
    rev_i = e_i[ln - 1::-1].transpose(1, 0, 2)[:, :, None, :]
    bt_r = bbar_r.transpose(0, 2, 1)[:, None]
    bt_i = bbar_i.transpose(0, 2, 1)[:, None]
    w_r = rev_r * bt_r - rev_i * bt_i
    w_i = rev_r * bt_i + rev_i * bt_r
    wst = jnp.concatenate([w_r, w_i, w_i, w_r], axis=-1).reshape(S5_GROUPS, S5_CHUNK_COLS, 4 * S5_STATE)
    nx_r = e_r[1:].transpose(1, 0, 2)[:, :, None, :]
    nx_i = e_i[1:].transpose(1, 0, 2)[:, :, None, :]
    ce_r = c_re[:, None] * nx_r - c_im[:, None] * nx_i
    ce_i = c_re[:, None] * nx_i + c_im[:, None] * nx_r
    wout = jnp.concatenate([ce_r, -ce_i], axis=-1)
    wout = wout.transpose(0, 3, 1, 2).reshape(S5_GROUPS, 2 * S5_STATE, S5_CHUNK_COLS)
    dl_r, dl_i = e_r[ln], e_i[ln]
    decay = jnp.stack([jnp.concatenate([dl_r, dl_r], -1),
                       jnp.concatenate([-dl_i, dl_i], -1),
                       jnp.concatenate([dl_i, -dl_i], -1)], axis=1)
    return kflat, wst.astype(BF16), wout.astype(BF16), decay


def _s5_conv_kernel(x_ref, kflat_ref, wst_ref, wout_ref, decay_ref, y_ref, sfin_ref,
                    toep_ref, sprev_ref, *, n_chunks, nb):
    kpad = jnp.concatenate([jnp.zeros((S5_GROUP, S5_CHUNK_COLS), F32), kflat_ref[0]], axis=1)
    for s in range(S5_CHUNK):
        lo = S5_CHUNK_COLS - S5_GROUP * s
        toep_ref[s * S5_GROUP:(s + 1) * S5_GROUP, :] = kpad[:, lo:lo + S5_CHUNK_COLS].astype(BF16)
    x = x_ref[0]
    sloc = jnp.dot(x, wst_ref[0], preferred_element_type=F32)
    d_a, d_b, d_bsw = decay_ref[0, 0:1, :], decay_ref[0, 1:2, :], decay_ref[0, 2:3, :]
    st = jnp.zeros((nb, 2 * S5_STATE), F32)
    st_sw = jnp.zeros((nb, 2 * S5_STATE), F32)
    for k in range(n_chunks):
        sprev_ref[k * nb:(k + 1) * nb, :] = st
        loc = sloc[k * nb:(k + 1) * nb, :]
        st, st_sw = (d_a * st + d_b * st_sw + loc[:, :2 * S5_STATE],
                     d_a * st_sw + d_bsw * st + loc[:, 2 * S5_STATE:])
    sfin_ref[0] = st
    y = jnp.dot(x, toep_ref[...], preferred_element_type=F32)
    y += jnp.dot(sprev_ref[...].astype(BF16), wout_ref[0], preferred_element_type=F32)
    y_ref[0] = y.astype(y_ref.dtype)


def _s5_conv(xg, kflat, wst, wout, decay, nb):
    rows = xg.shape[1]
    n_chunks = rows // nb
    kern = functools.partial(_s5_conv_kernel, n_chunks=n_chunks, nb=nb)
    return pl.pallas_call(
        kern,
        grid=(S5_GROUPS,),
        in_specs=[pl.BlockSpec((1, rows, S5_CHUNK_COLS), lambda g: (g, 0, 0)),
                  pl.BlockSpec((1, S5_GROUP, S5_CHUNK_COLS), lambda g: (g, 0, 0)),
                  pl.BlockSpec((1, S5_CHUNK_COLS, 4 * S5_STATE), lambda g: (g, 0, 0)),
                  pl.BlockSpec((1, 2 * S5_STATE, S5_CHUNK_COLS), lambda g: (g, 0, 0)),
                  pl.BlockSpec((1, 3, 2 * S5_STATE), lambda g: (g, 0, 0))],
        out_specs=[pl.BlockSpec((1, rows, S5_CHUNK_COLS), lambda g: (g, 0, 0)),
                   pl.BlockSpec((1, nb, 2 * S5_STATE), lambda g: (g, 0, 0))],
        out_shape=[jax.ShapeDtypeStruct((S5_GROUPS, rows, S5_CHUNK_COLS), BF16),
                   jax.ShapeDtypeStruct((S5_GROUPS, nb, 2 * S5_STATE), F32)],
        scratch_shapes=[pltpu.VMEM((S5_CHUNK_COLS, S5_CHUNK_COLS), BF16),
                        pltpu.VMEM((rows, 2 * S5_STATE), F32)],
        compiler_params=_cparams("parallel"),
        name="s5_conv",
    )(xg, kflat, wst, wout, decay)


def _s5_step_kernel(u_ref, s0_ref, s0sw_ref, bt_ref, dec_ref, cc_ref, y_ref, s_ref):
    for j in range(S5_SAMPLE_GROUPS_PER_STEP):
        bu = jnp.dot(u_ref[j], bt_ref[j], preferred_element_type=F32, precision=HIGHEST)
        s_new = dec_ref[j, 0:1, :] * s0_ref[j] + dec_ref[j, 1:2, :] * s0sw_ref[j] + bu
        s_ref[j] = s_new
        y_ref[j] = jnp.dot(s_new, cc_ref[j], preferred_element_type=F32, precision=HIGHEST)


def _s5_step(u, s0, s0sw, bt, dec, cc):
    nb = u.shape[1]
    gs = S5_SAMPLE_GROUPS_PER_STEP

    def spec(*tail):
        return pl.BlockSpec((gs,) + tail, lambda g: (g,) + (0,) * len(tail))

    return pl.pallas_call(
        _s5_step_kernel,
        grid=(S5_GROUPS // gs,),
        in_specs=[spec(nb, S5_GROUP), spec(nb, 2 * S5_STATE), spec(nb, 2 * S5_STATE),
                  spec(S5_GROUP, 2 * S5_STATE), spec(2, 2 * S5_STATE), spec(2 * S5_STATE, S5_GROUP)],
        out_specs=[spec(nb, S5_GROUP), spec(nb, 2 * S5_STATE)],
        out_shape=[jax.ShapeDtypeStruct((S5_GROUPS, nb, S5_GROUP), F32),
                   jax.ShapeDtypeStruct((S5_GROUPS, nb, 2 * S5_STATE), F32)],
        compiler_params=_cparams("parallel"),
        name="s5_step",
    )(u, s0, s0sw, bt, dec, cc)


def _compress_kernel(*refs, n_pages):
    pt_ref = refs[0]
    del pt_ref
    page_refs = refs[1:1 + n_pages]
    pos_ref, w1_ref, w2_ref, gain_ref, cos_ref, sin_ref, seg_ref, kc_ref, vc_ref, rows_ref = refs[1 + n_pages:]
    blocks_per_page = PAGE_SIZE // CMP_BLOCK
    n_blk = n_pages * blocks_per_page
    for p in range(n_pages):
        page = page_refs[p][0]
        for s in range(2):
            pos = jnp.concatenate([pos_ref[s]] * blocks_per_page, axis=0)
            for gi in range(N_KV_HEADS):
                off = (s * N_KV_HEADS + gi) * HEAD_DIM
                rows_ref[s, gi, p * PAGE_SIZE:(p + 1) * PAGE_SIZE, :] = page[:, off:off + HEAD_DIM] + pos
    for s in range(2):
        acc = jnp.zeros((N_KV_HEADS * n_blk, CMP_HIDDEN), F32)
        for l in range(CMP_BLOCK):
            a = rows_ref[s, :, pl.ds(l, n_blk, stride=CMP_BLOCK), :]
            a = a.reshape(N_KV_HEADS * n_blk, HEAD_DIM).astype(BF16)
            acc += jnp.dot(a, w1_ref[s, l], preferred_element_type=F32)
        hid = _gelu(acc).astype(BF16)
        comp = jnp.dot(hid, w2_ref[s], preferred_element_type=F32)
        comp = jnp.concatenate([comp[gi * n_blk:(gi + 1) * n_blk] for gi in range(N_KV_HEADS)], axis=1)
        if s == 0:
            kc_ref[0] = _head_norm_rope(comp, gain_ref[...], cos_ref[...], sin_ref[...], seg_ref[...])
        else:
            vc_ref[0] = comp


def _compress(rows_paged, page_table, cmp_pos, w1, w2, k_gain, cos_blk, sin_blk, seg):
    nb, n_pages = page_table.shape
    n_blk = n_pages * (PAGE_SIZE // CMP_BLOCK)

    def page_spec(p):
        return pl.BlockSpec((1, PAGE_SIZE, ROW_WIDTH), lambda b, pt: (pt[b, p], 0, 0))

    def full(shape):
        return pl.BlockSpec(shape, lambda b, pt: (0,) * len(shape))

    out = jax.ShapeDtypeStruct((nb, n_blk, KV_WIDTH), F32)
    out_spec = pl.BlockSpec((1, n_blk, KV_WIDTH), lambda b, pt: (b, 0, 0))
    grid_spec = pltpu.PrefetchScalarGridSpec(
        num_scalar_prefetch=1,
        grid=(nb,),
        in_specs=[page_spec(p) for p in range(n_pages)] + [
            full((2, CMP_BLOCK, HEAD_DIM)),
            full((2, CMP_BLOCK, HEAD_DIM, CMP_HIDDEN)),
            full((2, CMP_HIDDEN, HEAD_DIM)),
            full((1, KV_WIDTH)),
            full((n_blk, 2 * HEAD_DIM)), full((n_blk, 2 * HEAD_DIM)),
            full((KV_WIDTH, KV_WIDTH))],
        out_specs=[out_spec, out_spec],
        scratch_shapes=[pltpu.VMEM((2, N_KV_HEADS, n_pages * PAGE_SIZE, HEAD_DIM), F32)])
    gain = jnp.tile(k_gain.reshape(1, HEAD_DIM), (1, N_KV_HEADS))
    return pl.pallas_call(
        functools.partial(_compress_kernel, n_pages=n_pages),
        grid_spec=grid_spec,
        out_shape=[out, out],
        compiler_params=_cparams("arbitrary"),
        name="compress",
    )(page_table, *([rows_paged] * n_pages), cmp_pos, w1, w2, gain, cos_blk, sin_blk, seg)


def _select_blocks(imp, cur, n_rank):
    nj = imp.shape[0]
    j = lax.broadcasted_iota(jnp.int32, imp.shape, 0)
    causal = j <= cur
    forced = (j == 0) | (j == cur) | (j == cur - 1)
    score = jnp.where(causal, imp + jnp.where(forced, FORCE_SCORE, 0.0), -jnp.inf)
    rank = jnp.zeros(imp.shape, F32)
    for k in range(min(n_rank, nj)):
        row = score[k:k + 1, :]
        beats = (row > score) | ((row == score) & (j > k))
        rank += jnp.where(beats, 1.0, 0.0)
    return jnp.where(rank < float(N_SEL), 1.0, 0.0)


def _flash(qb, k_ref, v_ref, gi, tile_lo, tile_hi, tk, mask_fn, tq):
    def body(kt, carry):
        m, l, acc = carry
        k0 = pl.multiple_of(kt * tk, tk)
        kb = k_ref[0, gi, pl.ds(k0, tk), :]
        vb = v_ref[0, gi, pl.ds(k0, tk), :]
        s = lax.dot_general(qb, kb, NT_DIMS, preferred_element_type=F32).reshape(Q_PER_KV, tq, tk)
        ok = mask_fn(kt, k0)[None]
        s = jnp.where(ok, s, NEG_BIG)
        m_new = jnp.maximum(m, jnp.max(s, axis=-1, keepdims=True))
        p = jnp.where(ok, jnp.exp(s - m_new), 0.0)
        alpha = jnp.exp(m - m_new)
        l_new = alpha * l + jnp.sum(p, axis=-1, keepdims=True)
        pv = jnp.dot(p.reshape(Q_PER_KV * tq, tk).astype(BF16), vb, preferred_element_type=F32)
        return m_new, l_new, alpha * acc + pv.reshape(Q_PER_KV, tq, HEAD_DIM)

    init = (jnp.full((Q_PER_KV, tq, 1), NEG_BIG, F32), jnp.zeros((Q_PER_KV, tq, 1), F32),
            jnp.zeros((Q_PER_KV, tq, HEAD_DIM), F32))
    _, l, acc = lax.fori_loop(tile_lo, tile_hi, body, init)
    return acc / jnp.maximum(l, 1e-30)


def _attn_prompt_kernel(q_ref, gate_ref, kc_ref, vc_ref, sk_ref, sv_ref, wk_ref, wv_ref, e_ref, o_ref,
                        *, tq, tk_sel, tk_win, n_cmp):
    q0 = pl.program_id(1) * tq
    half = n_cmp // 2
    qpos = q0 + lax.broadcasted_iota(jnp.int32, (tq, 1), 0)
    qpos_lane = q0 + lax.broadcasted_iota(jnp.int32, (1, tq), 1)
    qpos_lane4 = jnp.concatenate([qpos_lane] * Q_PER_KV, axis=1)
    ci = lax.broadcasted_iota(jnp.int32, (n_cmp, 1), 0)
    cend_col = (2 * (ci % half) + ci // half + 1) * CMP_BLOCK - 1
    cj = lax.broadcasted_iota(jnp.int32, (1, n_cmp), 1)
    cend_row = (2 * (cj % half) + cj // half + 1) * CMP_BLOCK - 1
    gates = gate_ref[...]

    def gate_col(br, gi):
        cols = [gates[:, br * N_Q_HEADS + gi * Q_PER_KV + r:br * N_Q_HEADS + gi * Q_PER_KV + r + 1]
                for r in range(Q_PER_KV)]
        return jnp.stack(cols, axis=0)

    for gi in range(N_KV_HEADS):
        qf = q_ref[0, gi * Q_PER_KV:(gi + 1) * Q_PER_KV].reshape(Q_PER_KV * tq, HEAD_DIM)
        qb = qf.astype(BF16)
        kc = kc_ref[0, gi]
        vc = vc_ref[0, gi]
        s_c = lax.dot_general(qf, kc, NT_DIMS, preferred_element_type=F32, precision=HIGHEST)
        s_c = s_c.reshape(Q_PER_KV, tq, n_cmp)
        ok_c = (cend_row <= qpos)[None]
        s_c = jnp.where(ok_c, s_c, NEG_BIG)
        m_c = jnp.max(s_c, axis=-1, keepdims=True)
        p_c = jnp.where(ok_c, jnp.exp(s_c - m_c), 0.0)
        p_c = p_c / jnp.maximum(jnp.sum(p_c, axis=-1, keepdims=True), 1e-30)
        o_c = jnp.dot(p_c.reshape(Q_PER_KV * tq, n_cmp), vc, preferred_element_type=F32, precision=HIGHEST)
        o_c = o_c.reshape(Q_PER_KV, tq, HEAD_DIM)
        s_t = lax.dot_general(kc, qf, NT_DIMS, preferred_element_type=F32, precision=HIGHEST)
        ok_t = cend_col <= qpos_lane4
        s_t = jnp.where(ok_t, s_t, NEG_BIG)
        m_t = jnp.max(s_t, axis=0, keepdims=True)
        p_t = jnp.where(ok_t, jnp.exp(s_t - m_t), 0.0)
        p_t = p_t / jnp.maximum(jnp.sum(p_t, axis=0, keepdims=True), 1e-30)
        imp = p_t[:, 0:tq]
        for r in range(1, Q_PER_KV):
            imp = imp + p_t[:, r * tq:(r + 1) * tq]
        imp = imp[:half] + imp[half:]
        sel_t = _select_blocks(imp, qpos_lane // SEL_BLOCK, half).astype(BF16)

        def sel_mask(kt, k0, sel_t=sel_t):
            picked = lax.dot_general(sel_t, e_ref[kt], TN_DIMS, preferred_element_type=F32)
            kpos = k0 + lax.broadcasted_iota(jnp.int32, (1, tk_sel), 1)
            return (picked > 0.5) & (kpos <= qpos)

        def win_mask(kt, k0):
            dist = qpos - (k0 + lax.broadcasted_iota(jnp.int32, (1, tk_win), 1))
            return (dist >= 0) & (dist < WINDOW)

        o_s = _flash(qb, sk_ref, sv_ref, gi, 0, (q0 + tq + tk_sel - 1) // tk_sel, tk_sel, sel_mask, tq)
        win_lo = jnp.maximum(q0 - (WINDOW - 1), 0) // tk_win
        o_w = _flash(qb, wk_ref, wv_ref, gi, win_lo, (q0 + tq + tk_win - 1) // tk_win, tk_win, win_mask, tq)
        o = gate_col(0, gi) * o_c + gate_col(1, gi) * o_s + gate_col(2, gi) * o_w
        for r in range(Q_PER_KV):
            h0 = (gi * Q_PER_KV + r) * HEAD_DIM
            o_ref[0, :, h0:h0 + HEAD_DIM] = o[r].astype(o_ref.dtype)


def _attn_prompt(q, gates, kc, vc, sk, sv, wk, wv, tq=128, tk_sel=256, tk_win=128):
    nb, _, t, _ = q.shape
    n_cmp = kc.shape[2]
    n_sel_blocks = t // SEL_BLOCK
    key_blk = (jnp.arange(t) // SEL_BLOCK).reshape(t // tk_sel, 1, tk_sel)
    expand = (key_blk == jnp.arange(n_sel_blocks)[None, :, None]).astype(BF16)
    kern = functools.partial(_attn_prompt_kernel, tq=tq, tk_sel=tk_sel, tk_win=tk_win, n_cmp=n_cmp)
    kv_spec = pl.BlockSpec((1, N_KV_HEADS, t, HEAD_DIM), lambda b, i: (b, 0, 0, 0))
    c_spec = pl.BlockSpec((1, N_KV_HEADS, n_cmp, HEAD_DIM), lambda b, i: (b, 0, 0, 0))
    return pl.pallas_call(
        kern,
        grid=(nb, t // tq),
        in_specs=[pl.BlockSpec((1, N_Q_HEADS, tq, HEAD_DIM), lambda b, i: (b, 0, i, 0)),
                  pl.BlockSpec((tq, 128), lambda b, i: (b * (t // tq) + i, 0)),
                  c_spec, c_spec, kv_spec, kv_spec, kv_spec, kv_spec,
                  pl.BlockSpec((t // tk_sel, n_sel_blocks, tk_sel), lambda b, i: (0, 0, 0))],
        out_specs=pl.BlockSpec((1, tq, D_MODEL), lambda b, i: (b, i, 0)),
        out_shape=jax.ShapeDtypeStruct((nb, t, D_MODEL), BF16),
        compiler_params=_cparams("parallel", "arbitrary"),
        name="attn_prompt",
    )(q, gates, kc, vc, sk, sv, wk, wv, expand)


def _fold_heads(o):
    row_g = lax.broadcasted_iota(jnp.int32, o.shape, 0) // Q_PER_KV
    lane_g = lax.broadcasted_iota(jnp.int32, o.shape, 1) // HEAD_DIM
    o = jnp.where(row_g == lane_g, o, 0.0)
    out = o[:, :HEAD_DIM]
    for gi in range(1, N_KV_HEADS):
        out = out + o[:, gi * HEAD_DIM:(gi + 1) * HEAD_DIM]
    return out


def _masked_attend(s, ok, v):
    s = jnp.where(ok, s, NEG_BIG)
    m = jnp.max(s, axis=0, keepdims=True)
    p = jnp.where(ok, jnp.exp(s - m), 0.0)
    p = p / jnp.maximum(jnp.sum(p, axis=0, keepdims=True), 1e-30)
    o = lax.dot_general(p.astype(BF16), v, TN_DIMS, preferred_element_type=F32)
    return _fold_heads(o)


def _attn_sample_kernel(*refs, n_pages, past_len, win_buf):
    page_refs = refs[1:1 + n_pages]
    (qbd_ref, gate_ref, kc_ref, vc_ref, snew_ref, wcache_ref, wnew_ref, o_ref) = refs[1 + n_pages:]
    n_heads = N_Q_HEADS
    qbd = qbd_ref[0]
    qbd_b = qbd.astype(BF16)
    q_pos = past_len
    n_past_blk = past_len // SEL_BLOCK
    kc = kc_ref[0]
    n_cmp = kc.shape[0]
    s_c = jnp.dot(kc, qbd, preferred_element_type=F32, precision=HIGHEST)
    m_c = jnp.max(s_c, axis=0, keepdims=True)
    p_c = jnp.exp(s_c - m_c)
    p_c = p_c / jnp.maximum(jnp.sum(p_c, axis=0, keepdims=True), 1e-30)
    o_c = _fold_heads(lax.dot_general(p_c, vc_ref[0], TN_DIMS, preferred_element_type=F32, precision=HIGHEST))
    hr = lax.broadcasted_iota(jnp.int32, (n_heads, N_KV_HEADS), 0) // Q_PER_KV
    hc = lax.broadcasted_iota(jnp.int32, (n_heads, N_KV_HEADS), 1)
    group_sum = jnp.where(hr == hc, 1.0, 0.0)
    imp = jnp.dot(p_c, group_sum, preferred_element_type=F32, precision=HIGHEST)
    imp = imp[:n_cmp // 2] + imp[n_cmp // 2:]
    n_rows = n_past_blk + 8
    imp = jnp.concatenate([imp, jnp.zeros((8, N_KV_HEADS), F32)], axis=0)
    cur = jnp.full((1, N_KV_HEADS), q_pos // SEL_BLOCK, jnp.int32)
    sel = _select_blocks(imp, cur, n_past_blk + 1)
    gr = lax.broadcasted_iota(jnp.int32, (N_KV_HEADS, n_heads), 0)
    gc = lax.broadcasted_iota(jnp.int32, (N_KV_HEADS, n_heads), 1) // Q_PER_KV
    sel16 = jnp.dot(sel, jnp.where(gr == gc, 1.0, 0.0), preferred_element_type=F32)
    rows = jnp.concatenate([page_refs[p][0] for p in range(n_pages)]
                           + [jnp.concatenate([snew_ref[0], jnp.zeros((7, ROW_WIDTH), F32)], axis=0)], axis=0)
    rows_b = rows.astype(BF16)
    s_s = jnp.dot(rows_b[:, :KV_WIDTH], qbd_b, preferred_element_type=F32)
    past_ok = jnp.broadcast_to(sel16[:n_past_blk, None, :], (n_past_blk, SEL_BLOCK, n_heads))
    past_ok = past_ok.reshape(past_len, n_heads)
    tail_row = lax.broadcasted_iota(jnp.int32, (8, n_heads), 0)
    tail_ok = jnp.where(tail_row == 0, jnp.broadcast_to(sel16[n_past_blk:n_past_blk + 1], (8, n_heads)), 0.0)
    ok_s = jnp.concatenate([past_ok, tail_ok], axis=0) > 0.5
    o_s = _masked_attend(s_s, ok_s, rows_b[:, KV_WIDTH:])
    wrows = jnp.concatenate([wcache_ref[0], wnew_ref[0], jnp.zeros((7, ROW_WIDTH), F32)], axis=0).astype(BF16)
    s_w = jnp.dot(wrows[:, :KV_WIDTH], qbd_b, preferred_element_type=F32)
    wi = lax.broadcasted_iota(jnp.int32, (win_buf + 8, n_heads), 0)
    kw_pos = past_len - win_buf + wi
    dist = q_pos - kw_pos
    ok_w = (dist >= 0) & (dist < WINDOW) & (kw_pos >= 0) & (wi <= win_buf)
    o_w = _masked_attend(s_w, ok_w, wrows[:, KV_WIDTH:])
    g = gate_ref[0]
    o_ref[0] = g[:, 0:1] * o_c + g[:, 1:2] * o_s + g[:, 2:3] * o_w


def _attn_sample(page_table, cache_slc, qbd, gates_t, kc, vc, slc_new, cache_win, win_new):
    nb, n_pages = page_table.shape
    past_len = n_pages * PAGE_SIZE
    win_buf = cache_win.shape[1]
    n_cmp = kc.shape[1]

    def page_spec(p):
        return pl.BlockSpec((1, PAGE_SIZE, ROW_WIDTH), lambda b, pt: (pt[b, p], 0, 0))

    def per_b(*tail):
        return pl.BlockSpec((1,) + tail, lambda b, pt: (b,) + (0,) * len(tail))

    grid_spec = pltpu.PrefetchScalarGridSpec(
        num_scalar_prefetch=1,
        grid=(nb,),
        in_specs=[page_spec(p) for p in range(n_pages)] + [
            per_b(KV_WIDTH, N_Q_HEADS), per_b(N_Q_HEADS, 8), per_b(n_cmp, KV_WIDTH), per_b(n_cmp, KV_WIDTH),
            per_b(1, ROW_WIDTH), per_b(win_buf, ROW_WIDTH), per_b(1, ROW_WIDTH)],
        out_specs=per_b(N_Q_HEADS, HEAD_DIM))
    return pl.pallas_call(
        functools.partial(_attn_sample_kernel, n_pages=n_pages, past_len=past_len, win_buf=win_buf),
        grid_spec=grid_spec,
        out_shape=jax.ShapeDtypeStruct((nb, N_Q_HEADS, HEAD_DIM), F32),
        compiler_params=_cparams("arbitrary"),
        name="attn_sample",
    )(page_table, *([cache_slc] * n_pages), qbd, gates_t, kc, vc, slc_new, cache_win, win_new)


def _rope_tables(pos):
    half = HEAD_DIM // 2
    inv_freq = ROPE_THETA ** (-jnp.arange(half, dtype=F32) / half)
    ang = pos.astype(F32)[:, None] * inv_freq[None, :]
    cos, sin = jnp.cos(ang), jnp.sin(ang)
    cos_t = jnp.concatenate([cos, cos, cos, cos], axis=1)
    sin_t = jnp.concatenate([-sin, sin, -sin, sin], axis=1)
    return cos_t, sin_t


def _even_odd(a, axis):
    n = a.shape[axis]
    idx = jnp.concatenate([jnp.arange(0, n, 2), jnp.arange(1, n, 2)])
    return jnp.take(a, idx, axis=axis)


def _head_segments():
    i = jnp.arange(KV_WIDTH) // HEAD_DIM
    return (i[:, None] == i[None, :]).astype(BF16)


def kernel(x_prompt, x_sample, cache_cmp_kv, cache_slc_kv, cache_win_kv, state_s5, page_table, norm_mix, norm_mlp, w_up, w_down, s5_a_re, s5_a_im, s5_log_dt, s5_b_re, s5_b_im, s5_c_re, s5_c_im, s5_d, s5_w_glu, kv_norm, w_kv, k_norm, cmp_pos, cmp_w1, cmp_w2, nsa_w_in, q_norm, nsa_w_o):
    nb, t, _ = x_prompt.shape
    db, dt_len, _ = x_sample.shape
    assert dt_len == 1 and t % PAGE_SIZE == 0 and t % S5_CHUNK == 0
    n_pages = page_table.shape[1]
    past_len = n_pages * PAGE_SIZE
    n_a_layers = s5_a_re.shape[0]
    depth = norm_mix.shape[0]
    n_chunks = t // S5_CHUNK
    tm_p = 512
    seg = _head_segments()

    w_up_b, w_down_b = w_up.astype(BF16), w_down.astype(BF16)
    w_glu_b = s5_w_glu.astype(BF16)
    w_kv_b = w_kv.astype(BF16)
    qd = N_Q_HEADS * HEAD_DIM
    w_q_b = nsa_w_in[:, :, :qd].astype(BF16)
    w_gate = nsa_w_in[:, :, qd:].reshape(-1, D_MODEL, N_Q_HEADS, N_BRANCH).transpose(0, 1, 3, 2)
    w_gate = w_gate.reshape(-1, D_MODEL, N_BRANCH * N_Q_HEADS)
    w_gate_b = jnp.pad(w_gate, ((0, 0), (0, 0), (0, 128 - N_BRANCH * N_Q_HEADS))).astype(BF16)
    w_o_b = nsa_w_o.astype(BF16)
    w1_b = cmp_w1.reshape(2, CMP_BLOCK, HEAD_DIM, CMP_HIDDEN).astype(BF16)
    w2_b = cmp_w2.astype(BF16)

    xp = x_prompt.reshape(nb * t, D_MODEL)
    xs = x_sample.reshape(db, D_MODEL)
    s5_prompt, s5_sample = [], []
    for layer in range(n_a_layers):
        params = (s5_a_re[layer], s5_a_im[layer], s5_log_dt[layer], s5_b_re[layer], s5_b_im[layer],
                  s5_c_re[layer], s5_c_im[layer])
        kflat, wst, wout, decay = _s5_chunk_operators(*params)
        h = _norm_rows(xp, norm_mix[layer], tm_p, BF16)
        xg = h.reshape(nb, n_chunks, S5_CHUNK, S5_GROUPS, S5_GROUP).transpose(3, 1, 0, 2, 4)
        xg = xg.reshape(S5_GROUPS, n_chunks * nb, S5_CHUNK_COLS)
        yg, sfin = _s5_conv(xg, kflat, wst, wout, decay, nb)
        y = yg.reshape(S5_GROUPS, n_chunks, nb, S5_CHUNK, S5_GROUP).transpose(2, 1, 3, 0, 4)
        y = y.reshape(nb * t, D_MODEL)
        xp = _s5_out(xp, y, norm_mix[layer], s5_d[layer], w_glu_b[layer], tm_p)
        xp = _mlp(xp, norm_mlp[layer], w_up_b[layer], w_down_b[layer], tm_p, 512)
        s5_prompt.append(sfin.reshape(S5_GROUPS, nb, 2, S5_STATE).transpose(1, 0, 3, 2))
        _, abar_r, abar_i, bbar_r, bbar_i = _s5_discretize(*params[:5])
        bt = jnp.concatenate([bbar_r, bbar_i], axis=1).transpose(0, 2, 1)
        dec = jnp.stack([jnp.concatenate([abar_r, abar_r], -1), jnp.concatenate([-abar_i, abar_i], -1)], axis=1)
        cc = jnp.concatenate([s5_c_re[layer], -s5_c_im[layer]], axis=-1).transpose(0, 2, 1)
        hs = _norm_rows(xs, norm_mix[layer], db, F32)
        u = hs.reshape(db, S5_GROUPS, S5_GROUP).transpose(1, 0, 2)
        st0 = state_s5[layer].transpose(1, 0, 3, 2)
        s0 = st0.reshape(S5_GROUPS, db, 2 * S5_STATE)
        s0sw = st0[:, :, ::-1].reshape(S5_GROUPS, db, 2 * S5_STATE)
        ys, snew = _s5_step(u, s0, s0sw, bt, dec, cc)
        ys = ys.transpose(1, 0, 2).reshape(db, D_MODEL)
        xs = _s5_out(xs, ys, norm_mix[layer], s5_d[layer], w_glu_b[layer], db)
        xs = _mlp(xs, norm_mlp[layer], w_up_b[layer], w_down_b[layer], db, 512)
        s5_sample.append(snew.reshape(S5_GROUPS, db, 2, S5_STATE).transpose(1, 0, 3, 2))

    cos_p, sin_p = _rope_tables(jnp.arange(t))
    cos_s, sin_s = _rope_tables(jnp.full((db,), past_len))
    cmp_p, slc_p, win_p, sk, sv, wk, wv = _kv_rows(xp, nb, t, kv_norm, w_kv_b, k_norm, cos_p, sin_p, seg, tm_p)
    cmp_s, slc_s, win_s, _, _, _, _ = _kv_rows(xs, 1, db, kv_norm, w_kv_b, k_norm, cos_s, sin_s, seg, db)

    n_blk = t // CMP_BLOCK
    c_end = (jnp.arange(n_blk) + 1) * CMP_BLOCK - 1
    cos_c, sin_c = _rope_tables(c_end)
    ident_table = jnp.arange(nb * (t // PAGE_SIZE), dtype=jnp.int32).reshape(nb, t // PAGE_SIZE)
    kc_p, vc_p = _compress(cmp_p.reshape(-1, PAGE_SIZE, ROW_WIDTH), ident_table, cmp_pos, w1_b, w2_b,
                           k_norm[0], cos_c, sin_c, seg)

    def cmp_heads(a):
        a = _even_odd(a, 1)
        return a.reshape(a.shape[0], a.shape[1], N_KV_HEADS, HEAD_DIM).transpose(0, 2, 1, 3)

    kc_ph, vc_ph = cmp_heads(kc_p), cmp_heads(vc_p)

    n_blk_s = past_len // CMP_BLOCK
    cos_cs, sin_cs = _rope_tables((jnp.arange(n_blk_s) + 1) * CMP_BLOCK - 1)
    kc_s, vc_s = _compress(cache_cmp_kv.reshape(-1, PAGE_SIZE, ROW_WIDTH), page_table, cmp_pos, w1_b, w2_b,
                           k_norm[0], cos_cs, sin_cs, seg)
    kc_s, vc_s = _even_odd(kc_s, 1), _even_odd(vc_s, 1)
    cache_slc = cache_slc_kv.reshape(-1, PAGE_SIZE, ROW_WIDTH)
    cache_win = cache_win_kv.reshape(db, -1, ROW_WIDTH)
    eye_g = jnp.eye(N_KV_HEADS, dtype=F32)

    for layer in range(n_a_layers, depth):
        jb = layer - n_a_layers
        q, gates = _q_rows(xp, nb, t, norm_mix[layer], w_q_b[jb], w_gate_b[jb], q_norm[jb], cos_p, sin_p, seg, tm_p)
        att = _attn_prompt(q, gates, kc_ph, vc_ph, sk, sv, wk, wv)
        xp = _proj_residual(att.reshape(nb * t, D_MODEL), w_o_b[jb], xp, tm_p)
        xp = _mlp(xp, norm_mlp[layer], w_up_b[layer], w_down_b[layer], tm_p, 512)

        qs, gates_s = _q_rows(xs, 1, db, norm_mix[layer], w_q_b[jb], w_gate_b[jb], q_norm[jb], cos_s, sin_s, seg, db)
        qs = qs[0].transpose(1, 0, 2).reshape(db, N_KV_HEADS, Q_PER_KV, HEAD_DIM)
        qbd = jnp.einsum('bgrd,gh->bgdhr', qs, eye_g).reshape(db, KV_WIDTH, N_Q_HEADS)
        gates_t = gates_s[:, :N_BRANCH * N_Q_HEADS].reshape(db, N_BRANCH, N_Q_HEADS).transpose(0, 2, 1)
        gates_t = jnp.pad(gates_t, ((0, 0), (0, 0), (0, 8 - N_BRANCH)))
        att_s = _attn_sample(page_table, cache_slc, qbd, gates_t, kc_s, vc_s,
                             slc_s.reshape(db, 1, ROW_WIDTH), cache_win, win_s.reshape(db, 1, ROW_WIDTH))
        xs = _proj_residual(att_s.reshape(db, D_MODEL).astype(BF16), w_o_b[jb], xs, db)
        xs = _mlp(xs, norm_mlp[layer], w_up_b[layer], w_down_b[layer], db, 512)

    row_shape = (2, N_KV_HEADS, HEAD_DIM)
    win_keep = min(WINDOW, t)
    win_buf = cache_win_kv.shape[1]
    win_all = jnp.concatenate([cache_win_kv, win_s.reshape((db, 1) + row_shape)], axis=1)
    return (xp.reshape(nb, t, D_MODEL), xs.reshape(db, 1, D_MODEL),
            cmp_p.reshape((nb, t) + row_shape), cmp_s.reshape((db, 1) + row_shape),
            slc_p.reshape((nb, t) + row_shape), slc_s.reshape((db, 1) + row_shape),
            win_p.reshape((nb, t) + row_shape)[:, -win_keep:], win_all[:, -win_buf:],
            jnp.stack(s5_prompt), jnp.stack(s5_sample))
```
